```python
import math
import jax, jax.numpy as jnp
from jax import lax
import numpy as np

D_MODEL = 1024
BATCH = 32
SEQ = 2048
DEPTH = 1
DEC_BATCH = 128
DEC_SEQ = 8
PAST_LEN = 8192
PAGE_SIZE = 128

H_A = 8
DK_A = 128
DV_A = 128
D_A = H_A * DV_A
H_B = 8
DH_B = 128
D_B = H_B * DH_B
MOBA_BLOCK = 256
TOP_K = 3
QUERY_BLOCK = 128
CHUNK = 32
EPS = 1e-6
F32 = jnp.float32
SPLIT_WIDTHS = (H_A * DK_A, H_A * DK_A, D_A, D_A, D_B, D_B, D_B, D_B, D_MODEL, D_MODEL)
W_IN = sum(SPLIT_WIDTHS)
SPLITS = [int(s) for s in np.cumsum(SPLIT_WIDTHS)[:-1]]

kernel_name = 'hgrn2_moba_gated_hybrid_step'


def rmsnorm(x, g):
    x32 = x.astype(F32)
    y = x32 * lax.rsqrt(jnp.mean(x32 * x32, axis=-1, keepdims=True) + EPS)
    return (y * g.astype(F32)).astype(x.dtype)


def hgrn2_recurrence(q, k, v, log_f, s0):
    b_, t_, h_, dk = q.shape
    dv = v.shape[-1]
    c = math.gcd(t_, CHUNK)
    n = t_ // c

    def to_chunks(a):
        return a.reshape(b_, n, c, h_, a.shape[-1]).swapaxes(0, 1)

    causal = jnp.tril(jnp.ones((c, c), dtype=bool))[None, :, :, None, None]

    def step(s, inp):
        qc, kc, vc, gc = inp
        cum = jnp.cumsum(gc, axis=1)
        decay = jnp.exp(jnp.where(causal, cum[:, :, None] - cum[:, None, :], -jnp.inf))
        scores = jnp.einsum('bthd,bshd,btshd->bhts', qc, kc, decay)
        o = (jnp.einsum('bhts,bshv->bthv', scores, vc)
             + jnp.einsum('bthd,bhdv->bthv', qc * jnp.exp(cum), s))
        last = cum[:, -1]
        s_new = (jnp.exp(last)[..., None] * s
                 + jnp.einsum('bshd,bshv->bhdv', kc * jnp.exp(last[:, None] - cum), vc))
        return s_new, o

    s_final, o = lax.scan(step, s0, (to_chunks(q), to_chunks(k), to_chunks(v), to_chunks(log_f)))
    return o.swapaxes(0, 1).reshape(b_, t_, h_, dv), s_final


def moba_attend(q, k_sel, v_sel, sel_mask, k_own, v_own, own_mask):
    scale = DH_B ** -0.5
    q32 = q.astype(F32)
    s_own = jnp.einsum('qhd,khd->qhk', q32, k_own.astype(F32)) * scale
    s_own = jnp.where(own_mask[:, None, :], s_own, -jnp.inf)
    if k_sel is None:
        p = jax.nn.softmax(s_own, axis=-1)
        return jnp.einsum('qhk,khd->qhd', p, v_own.astype(F32)).astype(q.dtype)
    s_sel = jnp.einsum('qhd,qhkd->qhk', q32, k_sel.astype(F32)) * scale
    if sel_mask is not None:
        s_sel = jnp.where(sel_mask, s_sel, -jnp.inf)
    n_s = s_sel.shape[-1]
    p = jax.nn.softmax(jnp.concatenate([s_sel, s_own], axis=-1), axis=-1)
    o = (jnp.einsum('qhk,qhkd->qhd', p[..., :n_s], v_sel.astype(F32))
         + jnp.einsum('qhk,khd->qhd', p[..., n_s:], v_own.astype(F32)))
    return o.astype(q.dtype)


def moba_prompt(q, k, v):
    b_, t_, h_, dh = q.shape
    nb = -(-t_ // MOBA_BLOCK)
    pad = nb * MOBA_BLOCK - t_
    kb = jnp.pad(k, ((0, 0), (0, pad), (0, 0), (0, 0))).reshape(b_, nb, MOBA_BLOCK, h_, dh)
    vb = jnp.pad(v, ((0, 0), (0, pad), (0, 0), (0, 0))).reshape(b_, nb, MOBA_BLOCK, h_, dh)
    n_sel = min(TOP_K, nb - 1)
    nqb = t_ // QUERY_BLOCK
    b_idx = jnp.repeat(jnp.arange(b_), nqb)
    qb_idx = jnp.tile(jnp.arange(nqb), b_)
    q_items = q.reshape(b_ * nqb, QUERY_BLOCK, h_, dh)
    if n_sel > 0:
        means = jnp.mean(kb, axis=2, dtype=F32)
        scores = jnp.einsum('bthd,bnhd->bthn', q.astype(F32), means)
        q_blk = jnp.arange(t_) // MOBA_BLOCK
        past = jnp.arange(nb)[None, :] < q_blk[:, None]
        scores = jnp.where(past[None, :, None, :], scores, -jnp.inf)
        sel = lax.top_k(scores, n_sel)[1]
        ok = sel < q_blk[None, :, None, None]
        xs = (b_idx, qb_idx, q_items,
              sel.reshape(b_ * nqb, QUERY_BLOCK, h_, n_sel),
              ok.reshape(b_ * nqb, QUERY_BLOCK, h_, n_sel))
    else:
        xs = (b_idx, qb_idx, q_items)

    def item(args):
        bi, qbi, qi = args[0], args[1], args[2]
        k_seq, v_seq = kb[bi], vb[bi]
        j = (qbi * QUERY_BLOCK) // MOBA_BLOCK
        k_own = lax.dynamic_index_in_dim(k_seq, j, axis=0, keepdims=False)
        v_own = lax.dynamic_index_in_dim(v_seq, j, axis=0, keepdims=False)
        q_pos = qbi * QUERY_BLOCK + jnp.arange(QUERY_BLOCK)
        k_pos = j * MOBA_BLOCK + jnp.arange(MOBA_BLOCK)
        own_mask = k_pos[None, :] <= q_pos[:, None]
        if n_sel > 0:
            sel_i, ok_i = args[3], args[4]
            h_idx = jnp.arange(h_)[None, :, None]
            k_sel = k_seq[sel_i, :, h_idx, :].reshape(QUERY_BLOCK, h_, n_sel * MOBA_BLOCK, dh)
            v_sel = v_seq[sel_i, :, h_idx, :].reshape(QUERY_BLOCK, h_, n_sel * MOBA_BLOCK, dh)
            sel_mask = jnp.repeat(ok_i, MOBA_BLOCK, axis=-1)
        else:
            k_sel, v_sel, sel_mask = None, None, None
        return moba_attend(qi, k_sel, v_sel, sel_mask, k_own, v_own, own_mask)

    out = lax.map(item, xs)
    return out.reshape(b_, t_, h_, dh)


def moba_sample(q, k, v, cache_k, cache_v, page_table, layer_idx):
    db, ds, h_, dh = q.shape
    n_pages = page_table.shape[1]
    ppb = MOBA_BLOCK // PAGE_SIZE
    n_full = (n_pages * PAGE_SIZE) // MOBA_BLOCK
    own_start = n_full * ppb
    own_rows = (n_pages - own_start) * PAGE_SIZE
    n_sel = min(TOP_K, n_full)
    own_mask = jnp.concatenate([jnp.ones((ds, own_rows), dtype=bool),
                                jnp.tril(jnp.ones((ds, ds), dtype=bool))], axis=1)
    if n_sel > 0:
        page_mean = jnp.mean(cache_k, axis=2, dtype=F32)[layer_idx]
        bmean = page_mean[page_table[:, :own_start]].reshape(db, n_full, ppb, h_, dh).mean(axis=2)
        scores = jnp.einsum('bthd,bnhd->bthn', q.astype(F32), bmean)
        sel = lax.top_k(scores, n_sel)[1]
        xs = (page_table, q, k, v, sel)
    else:
        xs = (page_table, q, k, v)

    def item(args):
        pt, qi, ki, vi = args[0], args[1], args[2], args[3]
        own_pages = pt[own_start:]
        k_own = jnp.concatenate([cache_k[layer_idx, own_pages].reshape(own_rows, h_, dh), ki], axis=0)
        v_own = jnp.concatenate([cache_v[layer_idx, own_pages].reshape(own_rows, h_, dh), vi], axis=0)
        if n_sel > 0:
            sel_i = args[4]
            phys = pt[sel_i[..., None] * ppb + jnp.arange(ppb)]
            h_idx = jnp.arange(h_)[None, :, None, None]
            k_sel = cache_k[layer_idx, phys, :, h_idx, :].reshape(ds, h_, n_sel * MOBA_BLOCK, dh)
            v_sel = cache_v[layer_idx, phys, :, h_idx, :].reshape(ds, h_, n_sel * MOBA_BLOCK, dh)
        else:
            k_sel, v_sel = None, None
        return moba_attend(qi, k_sel, v_sel, None, k_own, v_own, own_mask)

    return lax.map(item, xs)


def layer(x, s0, attend, norm_w, w_in, lb, onorm_a, q_norm, k_norm, w_proj_a, w_proj_b, w_out):
    bsz, t_, _ = x.shape
    xn = rmsnorm(x, norm_w)
    h = xn @ w_in
    q_a, f_a, i_a, g_a, q_b, k_b, v_b, g_b, m_a, m_b = jnp.split(h, SPLITS, axis=-1)
    lbh = lb.reshape(H_A, DK_A)
    f_logit = f_a.astype(F32).reshape(bsz, t_, H_A, DK_A)
    log_f = jnp.log(lbh + (1.0 - lbh) * jax.nn.sigmoid(f_logit))
    key_a = (1.0 - lbh) * jax.nn.sigmoid(-f_logit)
    qry_a = jax.nn.silu(q_a.astype(F32)).reshape(bsz, t_, H_A, DK_A)
    val_a = i_a.astype(F32).reshape(bsz, t_, H_A, DV_A)
    o_a, s_new = hgrn2_recurrence(qry_a, key_a, val_a, log_f, s0)
    o_a = rmsnorm(o_a, onorm_a.reshape(H_A, DV_A)).reshape(bsz, t_, D_A).astype(x.dtype)
    u_a = (o_a * jax.nn.silu(g_a)) @ w_proj_a
    q = rmsnorm(q_b.reshape(bsz, t_, H_B, DH_B), q_norm)
    k = rmsnorm(k_b.reshape(bsz, t_, H_B, DH_B), k_norm)
    v = v_b.reshape(bsz, t_, H_B, DH_B)
    o_b = attend(q, k, v).reshape(bsz, t_, D_B)
    u_b = (o_b * jax.nn.silu(g_b)) @ w_proj_b
    merged = jax.nn.sigmoid(m_a) * u_a + jax.nn.sigmoid(m_b) * u_b
    y = x + merged @ w_out
    return y, k, v, s_new


def setup_inputs(seed: int = 0) -> dict:
    key = jax.random.key(seed)
    ks = jax.random.split(key, 16)
    n_pages = PAST_LEN // PAGE_SIZE
    n_used = DEC_BATCH * n_pages
    n_pool = n_used + max(1, n_used // 4)
    nrm = jax.random.normal
    x_prompt = nrm(ks[0], (BATCH, SEQ, D_MODEL), F32)
    x_sample = nrm(ks[1], (DEC_BATCH, DEC_SEQ, D_MODEL), F32)
    cache_k = nrm(ks[2], (DEPTH, n_pool, PAGE_SIZE, H_B, DH_B), F32)
    cache_v = nrm(ks[3], (DEPTH, n_pool, PAGE_SIZE, H_B, DH_B), F32)
    state_hgrn = 0.3 * nrm(ks[4], (DEPTH, DEC_BATCH, H_A, DK_A, DV_A), F32)
    page_table = jax.random.permutation(ks[5], n_pool)[:n_used].reshape(DEC_BATCH, n_pages).astype(jnp.int32)
    norm_w = 1.0 + 0.02 * nrm(ks[6], (DEPTH, D_MODEL), F32)
    w_in = nrm(ks[7], (DEPTH, D_MODEL, W_IN), F32) * D_MODEL ** -0.5
    lb_logits = 0.5 * nrm(ks[8], (DEPTH + 1, H_A * DK_A), F32)
    onorm_a = 1.0 + 0.02 * nrm(ks[9], (DEPTH, D_A), F32)
    q_norm = 1.0 + 0.02 * nrm(ks[10], (DEPTH, DH_B), F32)
    k_norm = 1.0 + 0.02 * nrm(ks[11], (DEPTH, DH_B), F32)
    w_proj_a = nrm(ks[12], (DEPTH, D_A, D_MODEL), F32) * D_A ** -0.5
    w_proj_b = nrm(ks[13], (DEPTH, D_B, D_MODEL), F32) * D_B ** -0.5
    w_out = nrm(ks[14], (DEPTH, D_MODEL, D_MODEL), F32) * D_MODEL ** -0.5
    return {'x_prompt': x_prompt, 'x_sample': x_sample, 'cache_k': cache_k, 'cache_v': cache_v,
            'state_hgrn': state_hgrn, 'page_table': page_table, 'norm_w': norm_w, 'w_in': w_in,
            'lb_logits': lb_logits, 'onorm_a': onorm_a, 'q_norm': q_norm, 'k_norm': k_norm,
            'w_proj_a': w_proj_a, 'w_proj_b': w_proj_b, 'w_out': w_out}


def reference(x_prompt, x_sample, cache_k, cache_v, state_hgrn, page_table, norm_w, w_in, lb_logits,
              onorm_a, q_norm, k_norm, w_proj_a, w_proj_b, w_out):
    lb_all = jnp.cumsum(jax.nn.softmax(lb_logits.astype(F32), axis=0), axis=0)
    y_p, y_s = x_prompt, x_sample
    kp_l, vp_l, sp_l, ks_l, vs_l, ss_l = [], [], [], [], [], []
    for l in range(DEPTH):
        params = (norm_w[l], w_in[l], lb_all[l], onorm_a[l], q_norm[l], k_norm[l],
                  w_proj_a[l], w_proj_b[l], w_out[l])
        s0_p = jnp.zeros((y_p.shape[0], H_A, DK_A, DV_A), F32)
        y_p, k_p, v_p, s_p = layer(y_p, s0_p, moba_prompt, *params)
        attend_s = lambda q, k, v, l=l: moba_sample(q, k, v, cache_k, cache_v, page_table, l)
        y_s, k_s, v_s, s_s = layer(y_s, state_hgrn[l].astype(F32), attend_s, *params)
        kp_l.append(k_p); vp_l.append(v_p); sp_l.append(s_p)
        ks_l.append(k_s); vs_l.append(v_s); ss_l.append(s_s)
    k_prompt = jnp.stack(kp_l)
    v_prompt = jnp.stack(vp_l)
    s_prompt = jnp.stack(sp_l).astype(x_prompt.dtype)
    k_sample = jnp.stack(ks_l)
    v_sample = jnp.stack(vs_l)
    s_sample = jnp.stack(ss_l).astype(state_hgrn.dtype)
    return (y_p, y_s, k_prompt, v_prompt, s_prompt, k_sample, v_sample, s_sample)
```

```python
import functools

import jax
import jax.numpy as jnp
from jax import lax
from jax.experimental import pallas as pl
from jax.experimental.pallas import tpu as pltpu

F32 = jnp.float32
BF16 = jnp.bfloat16
D = 1024
H = 8
DH = 128
MOBA_BLOCK = 256
PAGE = 128
TOP_K = 3
CHUNK = 32
EPS = 1e-6
NEG_INF = float("-inf")
SCALE = DH ** -0.5
VMEM_LIMIT = 48 * 1024 * 1024
NT = (((1,), (1,)), ((), ()))
TN = (((0,), (0,)), ((), ()))
HIGHEST = lax.Precision.HIGHEST


def _silu(y):
    return y * jax.nn.sigmoid(y)


def _params(*sem):
    return pltpu.CompilerParams(dimension_semantics=sem, vmem_limit_bytes=VMEM_LIMIT)


def _inproj_kernel(x_ref, nw_ref, w_ref, qn_ref, kn_ref, *out_refs, kinds):
    x = x_ref[...]
    ms = jnp.mean(x * x, axis=-1, keepdims=True)
    xn = (x * lax.rsqrt(ms + EPS) * nw_ref[...]).astype(BF16)
    for c, kind in enumerate(kinds):
        y = jnp.dot(xn, w_ref[:, c * D:(c + 1) * D], preferred_element_type=F32)
        o_ref = out_refs[c]
        if kind == "id":
            o_ref[...] = y
        elif kind == "silu":
            o_ref[...] = _silu(y)
        elif kind == "sigmoid":
            o_ref[...] = jax.nn.sigmoid(y)
        else:
            g = (qn_ref if kind == "qnorm" else kn_ref)[...]
            for h in range(H):
                yh = y[:, h * DH:(h + 1) * DH]
                msh = jnp.mean(yh * yh, axis=-1, keepdims=True)
                o_ref[:, h * DH:(h + 1) * DH] = yh * lax.rsqrt(msh + EPS) * g


def _inproj(x2d, nw, w_bf, qn, kn, first_split, kinds, tm):
    n = x2d.shape[0]
    nk = len(kinds)
    assert n % tm == 0 and first_split % nk == 0
    return pl.pallas_call(
        functools.partial(_inproj_kernel, kinds=kinds),
        grid=(n // tm,),
        in_specs=[
            pl.BlockSpec((tm, D), lambda i: (i, 0)),
            pl.BlockSpec((1, D), lambda i: (0, 0)),
            pl.BlockSpec((D, nk * D), lambda i: (0, first_split // nk)),
            pl.BlockSpec((1, DH), lambda i: (0, 0)),
            pl.BlockSpec((1, DH), lambda i: (0, 0)),
        ],
        out_specs=[pl.BlockSpec((tm, D), lambda i: (i, 0))] * nk,
        out_shape=[jax.ShapeDtypeStruct((n, D), F32)] * nk,
        compiler_params=_params("parallel"),
        name="inproj_%d" % first_split,
    )(x2d, nw, w_bf, qn, kn)


def _hgrn_kernel(*refs, tb, t_valid, has_s0):
    if has_s0:
        qa, fl, ia, ga, lbl, on, s0, og, sout, st = refs
    else:
        qa, fl, ia, ga, lbl, on, og, sout, st = refs
        s0 = None
    t = pl.program_id(1)

    @pl.when(t == 0)
    def _():
        st[...] = s0[...] if has_s0 else jnp.zeros(st.shape, F32)

    lg = lbl[...]
    e = jnp.exp(lg - jnp.max(lg, axis=0, keepdims=True))
    lb_all = e[0:1, :] / jnp.sum(e, axis=0, keepdims=True)
    row = lax.broadcasted_iota(jnp.int32, (CHUNK, CHUNK), 0)
    col = lax.broadcasted_iota(jnp.int32, (CHUNK, CHUNK), 1)
    tril = row >= col
    trilf = tril.astype(F32)

    def chunk(c, carry):
        rows = pl.ds(pl.multiple_of(c * CHUNK, CHUNK), CHUNK)
        for h in range(H):
            cols = slice(h * DH, (h + 1) * DH)
            lb = lb_all[:, cols]
            flc = fl[rows, cols]
            logf = jnp.log(lb + (1.0 - lb) * jax.nn.sigmoid(flc))
            key = (1.0 - lb) * jax.nn.sigmoid(-flc)
            if t_valid < tb:
                live = lax.broadcasted_iota(jnp.int32, (CHUNK, DH), 0) + c * CHUNK < t_valid
                logf = jnp.where(live, logf, 0.0)
                key = jnp.where(live, key, 0.0)
            cum = jnp.dot(trilf, logf, precision=HIGHEST, preferred_element_type=F32)
            last = cum[CHUNK - 1:CHUNK, :]
            v = ia[rows, cols].astype(BF16)
            qd = (qa[rows, cols] * jnp.exp(cum)).astype(BF16)
            kd = (key * jnp.exp(-cum)).astype(BF16)
            kl = (key * jnp.exp(last - cum)).astype(BF16)
            sc = lax.dot_general(qd, kd, NT, preferred_element_type=F32)
            sc = jnp.where(tril, sc, 0.0).astype(BF16)
            s_old = st[h]
            o = (jnp.dot(sc, v, preferred_element_type=F32)
                 + lax.dot_general(qd, s_old.astype(BF16), NT, preferred_element_type=F32))
            st[h] = s_old * jnp.exp(last) + lax.dot_general(v, kl, TN, preferred_element_type=F32)
            ms = jnp.mean(o * o, axis=-1, keepdims=True)
            og[rows, cols] = o * lax.rsqrt(ms + EPS) * on[:, cols] * ga[rows, cols]
        return carry

    lax.fori_loop(0, tb // CHUNK, chunk, 0)

    @pl.when(t == pl.num_programs(1) - 1)
    def _():
        sout[...] = st[...]


def _hgrn(qa, fl, ia, ga, lb_logits, onorm, s0t, bsz, t_len, tb, t_valid=None):
    assert t_len % tb == 0 and tb % CHUNK == 0
    nt = t_len // tb
    t_valid = tb if t_valid is None else t_valid
    assert t_valid == tb or nt == 1
    tok = pl.BlockSpec((tb, D), lambda b, t: (b * nt + t, 0))
    state = pl.BlockSpec((None, H, DH, DH), lambda b, t: (b, 0, 0, 0))
    in_specs = [tok, tok, tok, tok,
                pl.BlockSpec(lb_logits.shape, lambda b, t: (0, 0)),
                pl.BlockSpec((1, D), lambda b, t: (0, 0))]
    args = [qa, fl, ia, ga, lb_logits, onorm]
    if s0t is not None:
        in_specs.append(state)
        args.append(s0t)
    return pl.pallas_call(
        functools.partial(_hgrn_kernel, tb=tb, t_valid=t_valid, has_s0=s0t is not None),
        grid=(bsz, nt),
        in_specs=in_specs,
        out_specs=[tok, state],
        out_shape=[jax.ShapeDtypeStruct((bsz * t_len, D), F32),
                   jax.ShapeDtypeStruct((bsz, H, DH, DH), F32)],
        scratch_shapes=[pltpu.VMEM((H, DH, DH), F32)],
        compiler_params=_params("parallel", "arbitrary"),
        name="hgrn",
    )(*args)


def _select_bias(scores, valid, n_sel):
    nblk = scores.shape[-1]
    nidx = lax.broadcasted_iota(jnp.int32, scores.shape, 1)
    sm = scores if valid is None else jnp.where(valid, scores, NEG_INF)
    rank = jnp.zeros(scores.shape, jnp.int32)
    for m in range(nblk):
        cm = sm[:, m:m + 1]
        beats = jnp.logical_or(cm > sm, jnp.logical_and(cm == sm, nidx > m))
        rank = rank + beats.astype(jnp.int32)
    sel = rank < n_sel
    if valid is not None:
        sel = jnp.logical_and(sel, valid)
    return jnp.where(sel, 0.0, NEG_INF)


def _moba_prompt_kernel(q_ref, k_ref, v_ref, g_ref, o_ref, means_ref, *, nb, n_sel):
    qi = pl.program_id(2)
    mb = MOBA_BLOCK

    @pl.when(qi == 0)
    def _():
        for n in range(nb):
            means_ref[n:n + 1, :] = jnp.mean(k_ref[n * mb:(n + 1) * mb, :], axis=0, keepdims=True)

    q = q_ref[...]
    ssel = lax.dot_general(q, means_ref[...], NT, precision=HIGHEST, preferred_element_type=F32)
    nidx = lax.broadcasted_iota(jnp.int32, (mb, nb), 1)
    selb = _select_bias(ssel, nidx < qi, n_sel)
    qb = q.astype(BF16)
    r = lax.broadcasted_iota(jnp.int32, (mb, mb), 0)
    c = lax.broadcasted_iota(jnp.int32, (mb, mb), 1)
    causal = c <= r
    tiles = []
    for j in range(nb):
        kj = k_ref[j * mb:(j + 1) * mb, :].astype(BF16)
        s = lax.dot_general(qb, kj, NT, preferred_element_type=F32) * SCALE
        own = jnp.logical_and(causal, qi == j)
        tiles.append(jnp.where(own, s, s + selb[:, j:j + 1]))
    m = functools.reduce(jnp.maximum, [jnp.max(s, axis=-1, keepdims=True) for s in tiles])
    l = jnp.zeros((mb, 1), F32)
    acc = jnp.zeros((mb, DH), F32)
    for j in range(nb):
        p = jnp.exp(tiles[j] - m)
        l = l + jnp.sum(p, axis=-1, keepdims=True)
        acc = acc + jnp.dot(p.astype(BF16), v_ref[j * mb:(j + 1) * mb, :].astype(BF16),
                            preferred_element_type=F32)
    o_ref[...] = acc / l * g_ref[...]


def _moba_prompt(q, k, v, g, bsz, t_len):
    assert t_len % MOBA_BLOCK == 0
    nb = t_len // MOBA_BLOCK
    tok = pl.BlockSpec((MOBA_BLOCK, DH), lambda b, h, i: (b * nb + i, h))
    seq = pl.BlockSpec((None, t_len, DH), lambda b, h, i: (b, 0, h))
    return pl.pallas_call(
        functools.partial(_moba_prompt_kernel, nb=nb, n_sel=min(TOP_K, nb - 1)),
        grid=(bsz, H, nb),
        in_specs=[tok, seq, seq, tok],
        out_specs=tok,
        out_shape=jax.ShapeDtypeStruct((bsz * t_len, D), F32),
        scratch_shapes=[pltpu.VMEM((nb, DH), F32)],
        compiler_params=_params("parallel", "parallel", "arbitrary"),
        name="moba_prompt",
    )(q, k.reshape(bsz, t_len, D), v.reshape(bsz, t_len, D), g)


MEAN_PAGES = 16
ATTN_PAGES = 8


def _page_spec(c, per_step):
    return pl.BlockSpec((None, PAGE, H, DH), lambda b, i, pt: (pt[b, i * per_step + c], 0, 0, 0))


def _rows_by_heads(ref):
    return jnp.concatenate([ref[:, h, :] for h in range(H)], axis=-1)


def _block_mean_kernel(pt_ref, *refs):
    del pt_ref
    pages, o_ref = refs[:MEAN_PAGES], refs[MEAN_PAGES]
    ppb = MOBA_BLOCK // PAGE
    for n in range(MEAN_PAGES // ppb):
        tot = sum(jnp.sum(pages[n * ppb + c][...], axis=0) for c in range(ppb))
        o_ref[n] = tot * (1.0 / MOBA_BLOCK)


def _block_means(cache_k, page_table):
    db, n_pages = page_table.shape
    assert n_pages % MEAN_PAGES == 0
    ppb = MOBA_BLOCK // PAGE
    return pl.pallas_call(
        _block_mean_kernel,
        grid_spec=pltpu.PrefetchScalarGridSpec(
            num_scalar_prefetch=1,
            grid=(db, n_pages // MEAN_PAGES),
            in_specs=[_page_spec(c, MEAN_PAGES) for c in range(MEAN_PAGES)],
            out_specs=pl.BlockSpec((None, MEAN_PAGES // ppb, H, DH), lambda b, i, pt: (b, i, 0, 0)),
        ),
        out_shape=jax.ShapeDtypeStruct((db, n_pages // ppb, H, DH), F32),
        compiler_params=_params("parallel", "arbitrary"),
        name="block_means",
    )(page_table, *([cache_k] * MEAN_PAGES))


def _moba_sample_kernel(pt_ref, q_ref, ks_ref, vs_ref, g_ref, bm_ref, *rest, ds, n_full, n_sel):
    del pt_ref
    k_refs, v_refs = rest[:ATTN_PAGES], rest[ATTN_PAGES:2 * ATTN_PAGES]
    o_ref, qbd, selb, m_s, l_s, acc = rest[2 * ATTN_PAGES:]
    i = pl.program_id(1)
    nr = H * ds
    ppb = MOBA_BLOCK // PAGE

    @pl.when(i == 0)
    def _():
        q = q_ref[...]
        lane_h = lax.broadcasted_iota(jnp.int32, (ds, D), 1) // DH
        for h in range(H):
            qbd[h * ds:(h + 1) * ds, :] = jnp.where(lane_h == h, q, 0.0)
        qf = qbd[...]
        ssel = lax.dot_general(qf, _rows_by_heads(bm_ref), NT, precision=HIGHEST,
                               preferred_element_type=F32)
        selb[...] = _select_bias(ssel, None, n_sel)
        so = lax.dot_general(qf.astype(BF16), ks_ref[...].astype(BF16), NT,
                             preferred_element_type=F32) * SCALE
        trow = lax.broadcasted_iota(jnp.int32, (nr, ds), 0) % ds
        tcol = lax.broadcasted_iota(jnp.int32, (nr, ds), 1)
        so = jnp.where(tcol <= trow, so, NEG_INF)
        m0 = jnp.max(so, axis=-1, keepdims=True)
        p = jnp.exp(so - m0)
        vs = vs_ref[...]
        a = jnp.zeros((nr, D), F32)
        for t in range(ds):
            a = a + p[:, t:t + 1] * vs[t:t + 1, :]
        m_s[...] = m0
        l_s[...] = jnp.sum(p, axis=-1, keepdims=True)
        acc[...] = a

    qb = qbd[...].astype(BF16)
    nidx = lax.broadcasted_iota(jnp.int32, (nr, n_full), 1)
    sb = selb[...]
    for c in range(ATTN_PAGES):
        blk = (i * ATTN_PAGES + c) // ppb
        bias = jnp.max(jnp.where(nidx == blk, sb, NEG_INF), axis=-1, keepdims=True)
        kp = _rows_by_heads(k_refs[c]).astype(BF16)
        s = lax.dot_general(qb, kp, NT, preferred_element_type=F32) * SCALE + bias
        m_prev = m_s[...]
        m_new = jnp.maximum(m_prev, jnp.max(s, axis=-1, keepdims=True))
        alpha = jnp.exp(m_prev - m_new)
        p = jnp.exp(s - m_new)
        l_s[...] = alpha * l_s[...] + jnp.sum(p, axis=-1, keepdims=True)
        acc[...] = alpha * acc[...] + jnp.dot(p.astype(BF16), _rows_by_heads(v_refs[c]).astype(BF16),
                                              preferred_element_type=F32)
        m_s[...] = m_new

    @pl.when(i == pl.num_programs(1) - 1)
    def _():
        a = acc[...] / l_s[...]
        lane_h = lax.broadcasted_iota(jnp.int32, (ds, D), 1) // DH
        out = jnp.zeros((ds, D), F32)
        for h in range(H):
            out = out + jnp.where(lane_h == h, a[h * ds:(h + 1) * ds, :], 0.0)
        o_ref[...] = out * g_ref[...]


def _moba_sample(q, ks, vs, g, bmean, cache_k, cache_v, page_table, ds):
    db, n_pages = page_table.shape
    ppb = MOBA_BLOCK // PAGE
    assert n_pages % ATTN_PAGES == 0 and n_pages % ppb == 0 and ds % 8 == 0
    n_full = n_pages // ppb
    nr = H * ds
    seq = pl.BlockSpec((None, ds, D), lambda b, i, pt: (b, 0, 0))
    in_specs = [seq, seq, seq, seq, pl.BlockSpec((None, n_full, H, DH), lambda b, i, pt: (b, 0, 0, 0))]
    in_specs += [_page_spec(c, ATTN_PAGES) for c in range(ATTN_PAGES)] * 2
    r3 = lambda a: a.reshape(db, ds, D)
    out = pl.pallas_call(
        functools.partial(_moba_sample_kernel, ds=ds, n_full=n_full, n_sel=min(TOP_K, n_full)),
        grid_spec=pltpu.PrefetchScalarGridSpec(
            num_scalar_prefetch=1,
            grid=(db, n_pages // ATTN_PAGES),
            in_specs=in_specs,
            out_specs=seq,
            scratch_shapes=[pltpu.VMEM((nr, D), F32), pltpu.VMEM((nr, n_full), F32),
                            pltpu.VMEM((nr, 1), F32), pltpu.VMEM((nr, 1), F32),
                            pltpu.VMEM((nr, D), F32)],
        ),
        out_shape=jax.ShapeDtypeStruct((db, ds, D), F32),
        compiler_params=_params("parallel", "arbitrary"),
        name="moba_sample",
    )(page_table, r3(q), r3(ks), r3(vs), r3(g), bmean,
      *([cache_k] * ATTN_PAGES), *([cache_v] * ATTN_PAGES))
    return out.reshape(db * ds, D)


def _outproj_kernel(x_ref, oa_ref, ob_ref, ma_ref, mb_ref, wa_ref, wb_ref, wo_ref, y_ref):
    ua = jnp.dot(oa_ref[...].astype(BF16), wa_ref[...], preferred_element_type=F32)
    ub = jnp.dot(ob_ref[...].astype(BF16), wb_ref[...], preferred_element_type=F32)
    merged = ma_ref[...] * ua + mb_ref[...] * ub
    y_ref[...] = x_ref[...] + jnp.dot(merged.astype(BF16), wo_ref[...], preferred_element_type=F32)


def _outproj(x2d, oa, ob, ma, mb, wa, wb, wo, tm):
    n = x2d.shape[0]
    assert n % tm == 0
    tok = pl.BlockSpec((tm, D), lambda i: (i, 0))
    wsp = pl.BlockSpec((D, D), lambda i: (0, 0))
    return pl.pallas_call(
        _outproj_kernel,
        grid=(n // tm,),
        in_specs=[tok] * 5 + [wsp] * 3,
        out_specs=tok,
        out_shape=jax.ShapeDtypeStruct((n, D), F32),
        compiler_params=_params("parallel"),
        name="outproj",
    )(x2d, oa, ob, ma, mb, wa, wb, wo)


def _project_in(x2d, nw, w_bf, qn, kn, tm):
    qa, fl, ia, ga = _inproj(x2d, nw, w_bf, qn, kn, 0, ("silu", "id", "id", "silu"), tm)
    qb, k, v, gb = _inproj(x2d, nw, w_bf, qn, kn, 4, ("qnorm", "knorm", "id", "silu"), tm)
    ma, mb = _inproj(x2d, nw, w_bf, qn, kn, 8, ("sigmoid", "sigmoid"), tm)
    return qa, fl, ia, ga, qb, k, v, gb, ma, mb


def kernel(x_prompt, x_sample, cache_k, cache_v, state_hgrn, page_table, norm_w, w_in, lb_logits,
           onorm_a, q_norm, k_norm, w_proj_a, w_proj_b, w_out):
    assert w_in.shape[0] == 1 and lb_logits.shape[0] == 2, "single-layer step"
    bsz, t_len, _ = x_prompt.shape
    db, ds, _ = x_sample.shape
    n_pool = cache_k.shape[1]
    w_bf = w_in[0].astype(BF16)
    wa, wb, wo = (w[0].astype(BF16) for w in (w_proj_a, w_proj_b, w_out))
    nw, on, qn, kn = norm_w, onorm_a, q_norm, k_norm
    tm = 256

    xp = x_prompt.reshape(bsz * t_len, D)
    qa, fl, ia, ga, qb, k_p, v_p, gb, ma, mb = _project_in(xp, nw, w_bf, qn, kn, tm)
    oa, st_p = _hgrn(qa, fl, ia, ga, lb_logits, on, None, bsz, t_len, 256)
    ob = _moba_prompt(qb, k_p, v_p, gb, bsz, t_len)
    y_p = _outproj(xp, oa, ob, ma, mb, wa, wb, wo, tm).reshape(bsz, t_len, D)

    xs = x_sample.reshape(db * ds, D)
    tm = min(tm, db * ds)
    qa, fl, ia, ga, qb, k_s, v_s, gb, ma, mb = _project_in(xs, nw, w_bf, qn, kn, tm)
    assert ds <= CHUNK
    pad = lambda a: jnp.pad(a.reshape(db, ds, D), ((0, 0), (0, CHUNK - ds), (0, 0))).reshape(db * CHUNK, D)
    s0t = jnp.swapaxes(state_hgrn[0].astype(F32), -1, -2)
    oa, st_s = _hgrn(pad(qa), pad(fl), pad(ia), pad(ga), lb_logits, on, s0t, db, CHUNK, CHUNK, t_valid=ds)
    oa = oa.reshape(db, CHUNK, D)[:, :ds].reshape(db * ds, D)
    ck, cv = cache_k[0], cache_v[0]
    bmean = _block_means(ck, page_table)
    ob = _moba_sample(qb, k_s, v_s, gb, bmean, ck, cv, page_table, ds)
    y_s = _outproj(xs, oa, ob, ma, mb, wa, wb, wo, tm).reshape(db, ds, D)

    hd = lambda a, b_, t_: a.reshape(1, b_, t_, H, DH)
    return (y_p, y_s,
            hd(k_p, bsz, t_len), hd(v_p, bsz, t_len), jnp.swapaxes(st_p, -1, -2)[None],
            hd(k_s, db, ds), hd(v_s, db, ds),
            jnp.swapaxes(st_s, -1, -2)[None].astype(state_hgrn.dtype))
```

```python
import functools

import jax
import jax.numpy as jnp
from jax import lax
from jax.experimental import pallas as pl
from jax.experimental.pallas import tpu as pltpu

F32 = jnp.float32
BF16 = jnp.bfloat16
D = 1024
H = 8
DH = 128
MOBA_BLOCK = 256
PAGE = 128
TOP_K = 3
HGRN_CHUNK = 64
HGRN_GUARD = 75.0
EPS = 1e-6
NEG_INF = float("-inf")
SCALE = DH ** -0.5
VMEM_LIMIT = 48 * 1024 * 1024
NT = (((1,), (1,)), ((), ()))
TN = (((0,), (0,)), ((), ()))
HIGHEST = lax.Precision.HIGHEST


def _silu(y):
    return y * jax.nn.sigmoid(y)


def _params(*sem):
    return pltpu.CompilerParams(dimension_semantics=sem, vmem_limit_bytes=VMEM_LIMIT)


def _head(h):
    return slice(h * DH, (h + 1) * DH)


def _inproj_kernel(x_ref, nw_ref, w_ref, qn_ref, kn_ref, *out_refs, kinds):
    x = x_ref[...]
    ms = jnp.mean(x * x, axis=-1, keepdims=True)
    xn = (x * lax.rsqrt(ms + EPS) * nw_ref[...]).astype(BF16)
    for c, kind in enumerate(kinds):
        y = jnp.dot(xn, w_ref[:, c * D:(c + 1) * D], preferred_element_type=F32)
        o_ref = out_refs[c]
        if kind == "id":
            o_ref[...] = y
        elif kind == "silu":
            o_ref[...] = _silu(y)
        elif kind == "sigmoid":
            o_ref[...] = jax.nn.sigmoid(y)
        else:
            g = (qn_ref if kind == "qnorm" else kn_ref)[...]
            for h in range(H):
                yh = y[:, _head(h)]
                msh = jnp.mean(yh * yh, axis=-1, keepdims=True)
                o_ref[:, _head(h)] = yh * lax.rsqrt(msh + EPS) * g


def _inproj(x2d, nw, w_bf, qn, kn, first_split, kinds, tm):
    n = x2d.shape[0]
    nk = len(kinds)
    assert n % tm == 0 and first_split % nk == 0
    return pl.pallas_call(
        functools.partial(_inproj_kernel, kinds=kinds),
        grid=(n // tm,),
        in_specs=[
            pl.BlockSpec((tm, D), lambda i: (i, 0)),
            pl.BlockSpec((1, D), lambda i: (0, 0)),
            pl.BlockSpec((D, nk * D), lambda i: (0, first_split // nk)),
            pl.BlockSpec((1, DH), lambda i: (0, 0)),
            pl.BlockSpec((1, DH), lambda i: (0, 0)),
        ],
        out_specs=[pl.BlockSpec((tm, D), lambda i: (i, 0))] * nk,
        out_shape=[jax.ShapeDtypeStruct((n, D), F32)] * nk,
        compiler_params=_params("parallel"),
        name="inproj_%d" % first_split,
    )(x2d, nw, w_bf, qn, kn)


def _hgrn_kernel(*refs, rows_in, n_chunks, has_s0):
    if has_s0:
        qa, fl, ia, ga, lbl, on, s0, og, sout, st, obuf = refs
    else:
        qa, fl, ia, ga, lbl, on, og, sout, st, obuf = refs
        s0 = None
    cs = HGRN_CHUNK
    padded = rows_in < cs
    t = pl.program_id(1)

    @pl.when(t == 0)
    def _():
        st[...] = s0[...] if has_s0 else jnp.zeros(st.shape, F32)

    lg = lbl[...]
    e = jnp.exp(lg - jnp.max(lg, axis=0, keepdims=True))
    lb = e[0:1, :] / jnp.sum(e, axis=0, keepdims=True)
    omlb = 1.0 - lb
    row = lax.broadcasted_iota(jnp.int32, (cs, cs), 0)
    col = lax.broadcasted_iota(jnp.int32, (cs, cs), 1)
    tril = row >= col
    trilf = tril.astype(F32)
    rowid = lax.broadcasted_iota(jnp.int32, (cs, 1), 0)
    live = rowid < rows_in
    mid = cs // 2 - 1

    def chunk(c, carry):
        r0 = pl.multiple_of(c * cs, cs)

        def load(ref):
            if padded:
                return jnp.concatenate([ref[...], jnp.zeros((cs - rows_in, D), F32)], axis=0)
            return ref[pl.ds(r0, cs), :]

        flc = load(fl)
        logf = jnp.log(lb + omlb * jax.nn.sigmoid(flc))
        key = omlb * jax.nn.sigmoid(-flc)
        if padded:
            logf = jnp.where(live, logf, 0.0)
            key = jnp.where(live, key, 0.0)
        cum = jnp.dot(trilf, logf, precision=HIGHEST, preferred_element_type=F32)
        last = cum[cs - 1:cs, :]
        rel = cum - cum[mid:mid + 1, :]
        in_range = jnp.max(jnp.abs(rel)) <= HGRN_GUARD
        q = load(qa)
        v = load(ia)

        @pl.when(in_range)
        def _():
            qd = (q * jnp.exp(rel)).astype(BF16)
            kd = (key * jnp.exp(-rel)).astype(BF16)
            qs = (q * jnp.exp(cum)).astype(BF16)
            kl = (key * jnp.exp(last - cum)).astype(BF16)
            vb = v.astype(BF16)
            dl = jnp.exp(last)
            for h in range(H):
                hs = _head(h)
                sc = lax.dot_general(qd[:, hs], kd[:, hs], NT, preferred_element_type=F32)
                sc = jnp.where(tril, sc, 0.0).astype(BF16)
                s_old = st[h]
                obuf[:, hs] = (jnp.dot(sc, vb[:, hs], preferred_element_type=F32)
                               + lax.dot_general(qs[:, hs], s_old.astype(BF16), NT,
                                                 preferred_element_type=F32))
                st[h] = s_old * dl[:, hs] + lax.dot_general(vb[:, hs], kl[:, hs], TN,
                                                            preferred_element_type=F32)

        @pl.when(jnp.logical_not(in_range))
        def _():
            keyb = key.astype(BF16)
            obuf[...] = jnp.zeros(obuf.shape, F32)

            def tok(i, carry2):
                if padded:
                    fl_i = fl[pl.ds(jnp.minimum(i, rows_in - 1), 1), :]
                    f_i = jnp.where(i < rows_in, lb + omlb * jax.nn.sigmoid(fl_i), 1.0)
                else:
                    f_i = lb + omlb * jax.nn.sigmoid(fl[pl.ds(r0 + i, 1), :])
                only = rowid == i
                vm = jnp.where(only, v, 0.0).astype(BF16)
                qm = jnp.where(only, q, 0.0).astype(BF16)
                for h in range(H):
                    hs = _head(h)
                    s_new = st[h] * f_i[:, hs] + lax.dot_general(vm[:, hs], keyb[:, hs], TN,
                                                                 preferred_element_type=F32)
                    st[h] = s_new
                    obuf[:, hs] += lax.dot_general(qm[:, hs], s_new.astype(BF16), NT,
                                                   preferred_element_type=F32)
                return carry2

            lax.fori_loop(0, cs, tok, 0)

        gate = load(ga)
        for h in range(H):
            hs = _head(h)
            o = obuf[:, hs]
            ms = jnp.mean(o * o, axis=-1, keepdims=True)
            res = o * lax.rsqrt(ms + EPS) * on[:, hs] * gate[:, hs]
            if padded:
                og[:, hs] = res[:rows_in]
            else:
                og[pl.ds(r0, cs), hs] = res
        return carry

    lax.fori_loop(0, n_chunks, chunk, 0)

    @pl.when(t == pl.num_programs(1) - 1)
    def _():
        sout[...] = st[...]


def _hgrn(qa, fl, ia, ga, lb_logits, onorm, s0t, bsz, t_len, tb):
    assert t_len % tb == 0
    if tb < HGRN_CHUNK:
        assert t_len == tb and tb % 8 == 0
        n_chunks = 1
    else:
        assert tb % HGRN_CHUNK == 0
        n_chunks = tb // HGRN_CHUNK
    nt = t_len // tb
    tok = pl.BlockSpec((tb, D), lambda b, t: (b * nt + t, 0))
    state = pl.BlockSpec((None, H, DH, DH), lambda b, t: (b, 0, 0, 0))
    in_specs = [tok, tok, tok, tok,
                pl.BlockSpec(lb_logits.shape, lambda b, t: (0, 0)),
                pl.BlockSpec((1, D), lambda b, t: (0, 0))]
    args = [qa, fl, ia, ga, lb_logits, onorm]
    if s0t is not None:
        in_specs.append(state)
        args.append(s0t)
    return pl.pallas_call(
        functools.partial(_hgrn_kernel, rows_in=tb, n_chunks=n_chunks, has_s0=s0t is not None),
        grid=(bsz, nt),
        in_specs=in_specs,
        out_specs=[tok, state],
        out_shape=[jax.ShapeDtypeStruct((bsz * t_len, D), F32),
                   jax.ShapeDtypeStruct((bsz, H, DH, DH), F32)],
        scratch_shapes=[pltpu.VMEM((H, DH, DH), F32), pltpu.VMEM((HGRN_CHUNK, D), F32)],
        compiler_params=_params("parallel", "arbitrary"),
        name="hgrn",
    )(*args)


def _select_bias(scores, n_valid, n_sel, axis):
    nidx = lax.broadcasted_iota(jnp.int32, scores.shape, axis)
    valid = nidx < n_valid
    if n_valid <= n_sel:
        return jnp.where(valid, 0.0, NEG_INF)
    sm = jnp.where(valid, scores, NEG_INF)
    rank = jnp.zeros(scores.shape, jnp.int32)
    for m in range(n_valid):
        cm = lax.slice_in_dim(sm, m, m + 1, axis=axis)
        beats = jnp.logical_or(cm > sm, jnp.logical_and(cm == sm, nidx > m))
        rank = rank + beats.astype(jnp.int32)
    return jnp.where(jnp.logical_and(rank < n_sel, valid), 0.0, NEG_INF)


def _moba_prompt_kernel(q_ref, k_ref, v_ref, g_ref, o_ref, kb_ref, vb_ref, *, nb, n_sel):
    mb = MOBA_BLOCK
    kb_ref[...] = k_ref[...].astype(BF16)
    vb_ref[...] = v_ref[...].astype(BF16)
    means = jnp.concatenate(
        [jnp.mean(k_ref[n * mb:(n + 1) * mb, :], axis=0, keepdims=True) for n in range(nb)], axis=0)
    r = lax.broadcasted_iota(jnp.int32, (mb, mb), 0)
    c = lax.broadcasted_iota(jnp.int32, (mb, mb), 1)
    causal = c <= r
    for qi in range(nb):
        rows = slice(qi * mb, (qi + 1) * mb)
        q = q_ref[rows, :]
        qb = q.astype(BF16)
        s = lax.dot_general(qb, kb_ref[rows, :], NT, preferred_element_type=F32) * SCALE
        s = jnp.where(causal, s, NEG_INF)
        m = jnp.max(s, axis=-1, keepdims=True)
        p = jnp.exp(s - m)
        l = jnp.sum(p, axis=-1, keepdims=True)
        acc = jnp.dot(p.astype(BF16), vb_ref[rows, :], preferred_element_type=F32)
        if qi > 0 and n_sel > 0:
            ssel = lax.dot_general(q, means, NT, precision=HIGHEST, preferred_element_type=F32)
            selb = _select_bias(ssel, qi, n_sel, axis=1)
            for j in range(qi):
                kv = slice(j * mb, (j + 1) * mb)
                sj = lax.dot_general(qb, kb_ref[kv, :], NT, preferred_element_type=F32) * SCALE
                if qi > n_sel:
                    sj = sj + selb[:, j:j + 1]
                m_new = jnp.maximum(m, jnp.max(sj, axis=-1, keepdims=True))
                alpha = jnp.exp(m - m_new)
                p = jnp.exp(sj - m_new)
                l = alpha * l + jnp.sum(p, axis=-1, keepdims=True)
                acc = alpha * acc + jnp.dot(p.astype(BF16), vb_ref[kv, :], preferred_element_type=F32)
                m = m_new
        o_ref[rows, :] = acc / l * g_ref[rows, :]


def _moba_prompt(q, k, v, g, bsz, t_len):
    assert t_len % MOBA_BLOCK == 0
    nb = t_len // MOBA_BLOCK
    r3 = lambda a: a.reshape(bsz, t_len, D)
    seq = pl.BlockSpec((None, t_len, DH), lambda b, h: (b, 0, h))
    out = pl.pallas_call(
        functools.partial(_moba_prompt_kernel, nb=nb, n_sel=min(TOP_K, nb - 1)),
        grid=(bsz, H),
        in_specs=[seq, seq, seq, seq],
        out_specs=seq,
        out_shape=jax.ShapeDtypeStruct((bsz, t_len, D), F32),
        scratch_shapes=[pltpu.VMEM((t_len, DH), BF16), pltpu.VMEM((t_len, DH), BF16)],
        compiler_params=_params("parallel", "parallel"),
        name="moba_prompt",
    )(r3(q), r3(k), r3(v), r3(g))
    return out.reshape(bsz * t_len, D)


MEAN_PAGES = 16
ATTN_PAGES = 8
PPB = MOBA_BLOCK // PAGE


def _page_spec(c, per_step):
    return pl.BlockSpec((None, PAGE * H, DH), lambda b, i, pt: (pt[b, i * per_step + c], 0, 0))


def _head_rows(ref, h, n):
    return ref[pl.ds(h, n, stride=H), :]


def _block_mean_kernel(pt_ref, *refs):
    del pt_ref
    pages, o_ref = refs[:MEAN_PAGES], refs[MEAN_PAGES]
    for n in range(MEAN_PAGES // PPB):
        tot = sum(jnp.sum(pages[n * PPB + c][...].reshape(PAGE, H, DH), axis=0) for c in range(PPB))
        o_ref[n] = tot * (1.0 / MOBA_BLOCK)


def _block_means(cache_k, page_table):
    db, n_pages = page_table.shape
    assert n_pages % MEAN_PAGES == 0
    return pl.pallas_call(
        _block_mean_kernel,
        grid_spec=pltpu.PrefetchScalarGridSpec(
            num_scalar_prefetch=1,
            grid=(db, n_pages // MEAN_PAGES),
            in_specs=[_page_spec(c, MEAN_PAGES) for c in range(MEAN_PAGES)],
            out_specs=pl.BlockSpec((None, MEAN_PAGES // PPB, H, DH), lambda b, i, pt: (b, i, 0, 0)),
        ),
        out_shape=jax.ShapeDtypeStruct((db, n_pages // PPB, H, DH), F32),
        compiler_params=_params("parallel", "arbitrary"),
        name="block_means",
    )(page_table, *([cache_k] * MEAN_PAGES))


def _moba_sample_kernel(pt_ref, q_ref, ks_ref, vs_ref, g_ref, bm_ref, *rest, ds, n_full, n_sel):
    del pt_ref
    k_refs, v_refs = rest[:ATTN_PAGES], rest[ATTN_PAGES:2 * ATTN_PAGES]
    o_ref, qbd, selb, m_s, l_s, acc = rest[2 * ATTN_PAGES:]
    i = pl.program_id(1)
    nr = H * ds

    @pl.when(i == 0)
    def _():
        q = q_ref[...]
        lane_h = lax.broadcasted_iota(jnp.int32, (ds, D), 1) // DH
        for h in range(H):
            qbd[h * ds:(h + 1) * ds, :] = jnp.where(lane_h == h, q, 0.0)
        qf = qbd[...]
        bm = jnp.concatenate([_head_rows(bm_ref, h, n_full) for h in range(H)], axis=-1)
        ssel = lax.dot_general(qf, bm, NT, precision=HIGHEST, preferred_element_type=F32)
        selb[...] = _select_bias(ssel, n_full, n_sel, axis=1)
        so = lax.dot_general(qf.astype(BF16), ks_ref[...].astype(BF16), NT,
                             preferred_element_type=F32) * SCALE
        trow = lax.broadcasted_iota(jnp.int32, (nr, ds), 0) % ds
        tcol = lax.broadcasted_iota(jnp.int32, (nr, ds), 1)
        so = jnp.where(tcol <= trow, so, NEG_INF)
        m0 = jnp.max(so, axis=-1, keepdims=True)
        p = jnp.exp(so - m0)
        vs = vs_ref[...]
        for h in range(H):
            ph = p[h * ds:(h + 1) * ds, :]
            a = jnp.zeros((ds, DH), F32)
            for t in range(ds):
                a = a + ph[:, t:t + 1] * vs[t:t + 1, _head(h)]
            acc[h * ds:(h + 1) * ds, :] = a
        m_s[...] = m0
        l_s[...] = jnp.sum(p, axis=-1, keepdims=True)

    qb = qbd[...].astype(BF16)
    nidx = lax.broadcasted_iota(jnp.int32, (nr, n_full), 1)
    sb = selb[...]
    for c in range(0, ATTN_PAGES, PPB):
        blk = (i * ATTN_PAGES + c) // PPB
        bias = jnp.max(jnp.where(nidx == blk, sb, NEG_INF), axis=-1, keepdims=True)
        kp = jnp.concatenate(
            [jnp.concatenate([_head_rows(k_refs[c + u], h, PAGE) for h in range(H)], axis=-1)
             for u in range(PPB)], axis=0)
        s = lax.dot_general(qb, kp.astype(BF16), NT, preferred_element_type=F32) * SCALE + bias
        m_prev = m_s[...]
        m_new = jnp.maximum(m_prev, jnp.max(s, axis=-1, keepdims=True))
        alpha = jnp.exp(m_prev - m_new)
        p = jnp.exp(s - m_new)
        l_s[...] = alpha * l_s[...] + jnp.sum(p, axis=-1, keepdims=True)
        m_s[...] = m_new
        pb = p.astype(BF16)
        for h in range(H):
            hr = slice(h * ds, (h + 1) * ds)
            pv = sum(jnp.dot(pb[hr, u * PAGE:(u + 1) * PAGE],
                             _head_rows(v_refs[c + u], h, PAGE).astype(BF16),
                             preferred_element_type=F32) for u in range(PPB))
            acc[hr, :] = alpha[hr, :] * acc[hr, :] + pv

    @pl.when(i == pl.num_programs(1) - 1)
    def _():
        a = acc[...] / l_s[...]
        g = g_ref[...]
        for h in range(H):
            o_ref[:, _head(h)] = a[h * ds:(h + 1) * ds, :] * g[:, _head(h)]


def _moba_sample(q, ks, vs, g, bmean, cache_k, cache_v, page_table, ds):
    db, n_pages = page_table.shape
    assert n_pages % ATTN_PAGES == 0 and n_pages % PPB == 0 and ds % 8 == 0
    n_full = n_pages // PPB
    nr = H * ds
    seq = pl.BlockSpec((None, ds, D), lambda b, i, pt: (b, 0, 0))
    in_specs = [seq, seq, seq, seq, pl.BlockSpec((None, n_full * H, DH), lambda b, i, pt: (b, 0, 0))]
    in_specs += [_page_spec(c, ATTN_PAGES) for c in range(ATTN_PAGES)] * 2
    r3 = lambda a: a.reshape(db, ds, D)
    out = pl.pallas_call(
        functools.partial(_moba_sample_kernel, ds=ds, n_full=n_full, n_sel=min(TOP_K, n_full)),
        grid_spec=pltpu.PrefetchScalarGridSpec(
            num_scalar_prefetch=1,
            grid=(db, n_pages // ATTN_PAGES),
            in_specs=in_specs,
            out_specs=seq,
            scratch_shapes=[pltpu.VMEM((nr, D), F32), pltpu.VMEM((nr, n_full), F32),
                            pltpu.VMEM((nr, 1), F32), pltpu.VMEM((nr, 1), F32),
                            pltpu.VMEM((nr, DH), F32)],
        ),
        out_shape=jax.ShapeDtypeStruct((db, ds, D), F32),
        compiler_params=_params("parallel", "arbitrary"),
        name="moba_sample",
    )(page_table, r3(q), r3(ks), r3(vs), r3(g), bmean.reshape(db, n_full * H, DH),
      *([cache_k] * ATTN_PAGES), *([cache_v] * ATTN_PAGES))
    return out.reshape(db * ds, D)


def _outproj_kernel(x_ref, oa_ref, ob_ref, ma_ref, mb_ref, wa_ref, wb_ref, wo_ref, y_ref):
    ua = jnp.dot(oa_ref[...].astype(BF16), wa_ref[...], preferred_element_type=F32)
    ub = jnp.dot(ob_ref[...].astype(BF16), wb_ref[...], preferred_element_type=F32)
    merged = ma_ref[...] * ua + mb_ref[...] * ub
    y_ref[...] = x_ref[...] + jnp.dot(merged.astype(BF16), wo_ref[...], preferred_element_type=F32)


def _outproj(x2d, oa, ob, ma, mb, wa, wb, wo, tm):
    n = x2d.shape[0]
    assert n % tm == 0
    tok = pl.BlockSpec((tm, D), lambda i: (i, 0))
    wsp = pl.BlockSpec((D, D), lambda i: (0, 0))
    return pl.pallas_call(
        _outproj_kernel,
        grid=(n // tm,),
        in_specs=[tok] * 5 + [wsp] * 3,
        out_specs=tok,
        out_shape=jax.ShapeDtypeStruct((n, D), F32),
        compiler_params=_params("parallel"),
        name="outproj",
    )(x2d, oa, ob, ma, mb, wa, wb, wo)


def _project_in(x2d, nw, w_bf, qn, kn, tm):
    qa, fl, ia, ga = _inproj(x2d, nw, w_bf, qn, kn, 0, ("silu", "id", "id", "silu"), tm)
    qb, k, v, gb = _inproj(x2d, nw, w_bf, qn, kn, 4, ("qnorm", "knorm", "id", "silu"), tm)
    ma, mb = _inproj(x2d, nw, w_bf, qn, kn, 8, ("sigmoid", "sigmoid"), tm)
    return qa, fl, ia, ga, qb, k, v, gb, ma, mb


def kernel(x_prompt, x_sample, cache_k, cache_v, state_hgrn, page_table, norm_w, w_in, lb_logits,
           onorm_a, q_norm, k_norm, w_proj_a, w_proj_b, w_out):
    assert w_in.shape[0] == 1 and lb_logits.shape[0] == 2, "single-layer step"
    bsz, t_len, _ = x_prompt.shape
    db, ds, _ = x_sample.shape
    n_pool = cache_k.shape[1]
    w_bf = w_in[0].astype(BF16)
    wa, wb, wo = (w[0].astype(BF16) for w in (w_proj_a, w_proj_b, w_out))
    nw, on, qn, kn = norm_w, onorm_a, q_norm, k_norm
    tm = 256

    xp = x_prompt.reshape(bsz * t_len, D)
    qa, fl, ia, ga, qb, k_p, v_p, gb, ma, mb = _project_in(xp, nw, w_bf, qn, kn, tm)
    oa, st_p = _hgrn(qa, fl, ia, ga, lb_logits, on, None, bsz, t_len, 256)
    ob = _moba_prompt(qb, k_p, v_p, gb, bsz, t_len)
    y_p = _outproj(xp, oa, ob, ma, mb, wa, wb, wo, tm).reshape(bsz, t_len, D)

    xs = x_sample.reshape(db * ds, D)
    tm = min(tm, db * ds)
    qa, fl, ia, ga, qb, k_s, v_s, gb, ma, mb = _project_in(xs, nw, w_bf, qn, kn, tm)
    s0t = jnp.swapaxes(state_hgrn[0].astype(F32), -1, -2)
    oa, st_s = _hgrn(qa, fl, ia, ga, lb_logits, on, s0t, db, ds, ds)
    ck = cache_k[0].reshape(n_pool, PAGE * H, DH)
    cv = cache_v[0].reshape(n_pool, PAGE * H, DH)
    bmean = _block_means(ck, page_table)
    ob = _moba_sample(qb, k_s, v_s, gb, bmean, ck, cv, page_table, ds)
    y_s = _outproj(xs, oa, ob, ma, mb, wa, wb, wo, tm).reshape(db, ds, D)

    hd = lambda a, b_, t_: a.reshape(1, b_, t_, H, DH)
    return (y_p, y_s,
            hd(k_p, bsz, t_len), hd(v_p, bsz, t_len), jnp.swapaxes(st_p, -1, -2)[None],
            hd(k_s, db, ds), hd(v_s, db, ds),
            jnp.swapaxes(st_s, -1, -2)[None].astype(state_hgrn.dtype))
```

```python
import functools

import jax
import jax.numpy as jnp
from jax import lax
from jax.experimental import pallas as pl
from jax.experimental.pallas import tpu as pltpu

F32 = jnp.float32
BF16 = jnp.bfloat16
D = 1024
H = 8
DH = 128
MOBA_BLOCK = 256
PAGE = 128
TOP_K = 3
HGRN_CHUNK = 64
HGRN_GUARD = 75.0
EPS = 1e-6
NEG_INF = float("-inf")
SCALE = DH ** -0.5
VMEM_LIMIT = 48 * 1024 * 1024
PROMPT_ROWS = 512
SAMPLE_ROWS = 256
HGRN_ROWS = 256
NT = (((1,), (1,)), ((), ()))
TN = (((0,), (0,)), ((), ()))
HIGHEST = lax.Precision.HIGHEST


def _silu(y):
    return y * jax.nn.sigmoid(y)


def _params(*sem):
    return pltpu.CompilerParams(dimension_semantics=sem, vmem_limit_bytes=VMEM_LIMIT)


def _head(h):
    return slice(h * DH, (h + 1) * DH)


def _inproj_kernel(x_ref, nw_ref, w_ref, qn_ref, kn_ref, *out_refs, kinds):
    x = x_ref[...]
    ms = jnp.mean(x * x, axis=-1, keepdims=True)
    xn = (x * lax.rsqrt(ms + EPS) * nw_ref[...]).astype(BF16)
    for c, kind in enumerate(kinds):
        y = jnp.dot(xn, w_ref[:, c * D:(c + 1) * D], preferred_element_type=F32)
        o_ref = out_refs[c]
        if kind == "id":
            o_ref[...] = y.astype(o_ref.dtype)
        elif kind == "silu":
            o_ref[...] = _silu(y).astype(o_ref.dtype)
        elif kind == "sigmoid":
            o_ref[...] = jax.nn.sigmoid(y).astype(o_ref.dtype)
        else:
            g = (qn_ref if kind == "qnorm" else kn_ref)[...]
            for h in range(H):
                yh = y[:, _head(h)]
                msh = jnp.mean(yh * yh, axis=-1, keepdims=True)
                o_ref[:, _head(h)] = (yh * lax.rsqrt(msh + EPS) * g).astype(o_ref.dtype)


def _inproj(x2d, nw, w_bf, qn, kn, first_split, kinds, dtypes, tm):
    n = x2d.shape[0]
    nk = len(kinds)
    assert n % tm == 0 and first_split % nk == 0 and len(dtypes) == nk
    return pl.pallas_call(
        functools.partial(_inproj_kernel, kinds=kinds),
        grid=(n // tm,),
        in_specs=[
            pl.BlockSpec((tm, D), lambda i: (i, 0)),
            pl.BlockSpec((1, D), lambda i: (0, 0)),
            pl.BlockSpec((D, nk * D), lambda i: (0, first_split // nk)),
            pl.BlockSpec((1, DH), lambda i: (0, 0)),
            pl.BlockSpec((1, DH), lambda i: (0, 0)),
        ],
        out_specs=[pl.BlockSpec((tm, D), lambda i: (i, 0))] * nk,
        out_shape=[jax.ShapeDtypeStruct((n, D), dt) for dt in dtypes],
        compiler_params=_params("parallel"),
        name="inproj_%d" % first_split,
    )(x2d, nw, w_bf, qn, kn)


def _hgrn_kernel(*refs, rows_in, n_chunks, has_s0):
    if has_s0:
        qa, fl, ia, ga, lbl, on, s0, og, sout, st, obuf = refs
    else:
        qa, fl, ia, ga, lbl, on, og, sout, st, obuf = refs
        s0 = None
    cs = HGRN_CHUNK
    padded = rows_in < cs
    t = pl.program_id(1)

    @pl.when(t == 0)
    def _():
        st[...] = s0[...] if has_s0 else jnp.zeros(st.shape, F32)

    lg = lbl[...]
    e = jnp.exp(lg - jnp.max(lg, axis=0, keepdims=True))
    lb = e[0:1, :] / jnp.sum(e, axis=0, keepdims=True)
    omlb = 1.0 - lb
    row = lax.broadcasted_iota(jnp.int32, (cs, cs), 0)
    col = lax.broadcasted_iota(jnp.int32, (cs, cs), 1)
    tril = row >= col
    trilf = tril.astype(F32)
    rowid = lax.broadcasted_iota(jnp.int32, (cs, 1), 0)
    live = rowid < rows_in
    mid = cs // 2 - 1

    def chunk(c, carry):
        r0 = pl.multiple_of(c * cs, cs)

        def load(ref):
            if padded:
                return jnp.concatenate([ref[...].astype(F32), jnp.zeros((cs - rows_in, D), F32)], axis=0)
            return ref[pl.ds(r0, cs), :].astype(F32)

        flc = load(fl)
        logf = jnp.log(lb + omlb * jax.nn.sigmoid(flc))
        key = omlb * jax.nn.sigmoid(-flc)
        if padded:
            logf = jnp.where(live, logf, 0.0)
            key = jnp.where(live, key, 0.0)
        cum = jnp.dot(trilf, logf, precision=HIGHEST, preferred_element_type=F32)
        last = cum[cs - 1:cs, :]
        rel = cum - cum[mid:mid + 1, :]
        in_range = jnp.max(jnp.abs(rel)) <= HGRN_GUARD
        q = load(qa)
        v = load(ia)

        @pl.when(in_range)
        def _():
            qd = (q * jnp.exp(rel)).astype(BF16)
            kd = (key * jnp.exp(-rel)).astype(BF16)
            qs = (q * jnp.exp(cum)).astype(BF16)
            kl = (key * jnp.exp(last - cum)).astype(BF16)
            vb = v.astype(BF16)
            dl = jnp.exp(last)
            for h in range(H):
                hs = _head(h)
                sc = lax.dot_general(qd[:, hs], kd[:, hs], NT, preferred_element_type=F32)
                sc = jnp.where(tril, sc, 0.0).astype(BF16)
                s_old = st[h]
                obuf[:, hs] = (jnp.dot(sc, vb[:, hs], preferred_element_type=F32)
                               + lax.dot_general(qs[:, hs], s_old.astype(BF16), NT,
                                                 preferred_element_type=F32))
                st[h] = s_old * dl[:, hs] + lax.dot_general(vb[:, hs], kl[:, hs], TN,
                                                            preferred_element_type=F32)

        @pl.when(jnp.logical_not(in_range))
        def _():
            keyb = key.astype(BF16)
            obuf[...] = jnp.zeros(obuf.shape, F32)

            def tok(i, carry2):
                if padded:
                    fl_i = fl[pl.ds(jnp.minimum(i, rows_in - 1), 1), :]
                    f_i = jnp.where(i < rows_in, lb + omlb * jax.nn.sigmoid(fl_i), 1.0)
                else:
                    f_i = lb + omlb * jax.nn.sigmoid(fl[pl.ds(r0 + i, 1), :])
                only = rowid == i
                vm = jnp.where(only, v, 0.0).astype(BF16)
                qm = jnp.where(only, q, 0.0).astype(BF16)
                for h in range(H):
                    hs = _head(h)
                    s_new = st[h] * f_i[:, hs] + lax.dot_general(vm[:, hs], keyb[:, hs], TN,
                                                                 preferred_element_type=F32)
                    st[h] = s_new
                    obuf[:, hs] += lax.dot_general(qm[:, hs], s_new.astype(BF16), NT,
                                                   preferred_element_type=F32)
                return carry2

            lax.fori_loop(0, cs, tok, 0)

        gate = load(ga)
        for h in range(H):
            hs = _head(h)
            o = obuf[:, hs]
            ms = jnp.mean(o * o, axis=-1, keepdims=True)
            res = o * lax.rsqrt(ms + EPS) * on[:, hs] * gate[:, hs]
            if padded:
                og[:, hs] = res[:rows_in].astype(og.dtype)
            else:
                og[pl.ds(r0, cs), hs] = res.astype(og.dtype)
        return carry

    lax.fori_loop(0, n_chunks, chunk, 0)

    @pl.when(t == pl.num_programs(1) - 1)
    def _():
        sout[...] = st[...]


def _hgrn(qa, fl, ia, ga, lb_logits, onorm, s0t, bsz, t_len, tb):
    assert t_len % tb == 0
    if tb < HGRN_CHUNK:
        assert t_len == tb and tb % 8 == 0
        n_chunks = 1
    else:
        assert tb % HGRN_CHUNK == 0
        n_chunks = tb // HGRN_CHUNK
    nt = t_len // tb
    tok = pl.BlockSpec((tb, D), lambda b, t: (b * nt + t, 0))
    state = pl.BlockSpec((None, H, DH, DH), lambda b, t: (b, 0, 0, 0))
    in_specs = [tok, tok, tok, tok,
                pl.BlockSpec(lb_logits.shape, lambda b, t: (0, 0)),
                pl.BlockSpec((1, D), lambda b, t: (0, 0))]
    args = [qa, fl, ia, ga, lb_logits, onorm]
    if s0t is not None:
        in_specs.append(state)
        args.append(s0t)
    return pl.pallas_call(
        functools.partial(_hgrn_kernel, rows_in=tb, n_chunks=n_chunks, has_s0=s0t is not None),
        grid=(bsz, nt),
        in_specs=in_specs,
        out_specs=[tok, state],
        out_shape=[jax.ShapeDtypeStruct((bsz * t_len, D), qa.dtype),
                   jax.ShapeDtypeStruct((bsz, H, DH, DH), F32)],
        scratch_shapes=[pltpu.VMEM((H, DH, DH), F32), pltpu.VMEM((HGRN_CHUNK, D), F32)],
        compiler_params=_params("parallel", "arbitrary"),
        name="hgrn",
    )(*args)


def _select_bias(scores, n_valid, n_sel, axis):
    nidx = lax.broadcasted_iota(jnp.int32, scores.shape, axis)
    valid = nidx < n_valid
    if n_valid <= n_sel:
        return jnp.where(valid, 0.0, NEG_INF)
    sm = jnp.where(valid, scores, NEG_INF)
    rank = jnp.zeros(scores.shape, jnp.int32)
    for m in range(n_valid):
        cm = lax.slice_in_dim(sm, m, m + 1, axis=axis)
        beats = jnp.logical_or(cm > sm, jnp.logical_and(cm == sm, nidx > m))
        rank = rank + beats.astype(jnp.int32)
    return jnp.where(jnp.logical_and(rank < n_sel, valid), 0.0, NEG_INF)


def _moba_prompt_kernel(q_ref, k_ref, v_ref, g_ref, o_ref, kb_ref, vb_ref, *, nb, n_sel):
    mb = MOBA_BLOCK
    kb_ref[...] = k_ref[...].astype(BF16)
    vb_ref[...] = v_ref[...].astype(BF16)
    means = jnp.concatenate(
        [jnp.mean(k_ref[n * mb:(n + 1) * mb, :], axis=0, keepdims=True) for n in range(nb)], axis=0)
    r = lax.broadcasted_iota(jnp.int32, (mb, mb), 0)
    c = lax.broadcasted_iota(jnp.int32, (mb, mb), 1)
    causal = c <= r
    for qi in range(nb):
        rows = slice(qi * mb, (qi + 1) * mb)
        qb = q_ref[rows, :].astype(BF16)
        s = lax.dot_general(qb, kb_ref[rows, :], NT, preferred_element_type=F32) * SCALE
        s = jnp.where(causal, s, NEG_INF)
        m = jnp.max(s, axis=-1, keepdims=True)
        p = jnp.exp(s - m)
        l = jnp.sum(p, axis=-1, keepdims=True)
        acc = jnp.dot(p.astype(BF16), vb_ref[rows, :], preferred_element_type=F32)
        if qi > 0 and n_sel > 0:
            ssel = lax.dot_general(q_ref[rows, :].astype(F32), means, NT, precision=HIGHEST,
                                   preferred_element_type=F32)
            selb = _select_bias(ssel, qi, n_sel, axis=1)
            for j in range(qi):
                kv = slice(j * mb, (j + 1) * mb)
                sj = lax.dot_general(qb, kb_ref[kv, :], NT, preferred_element_type=F32) * SCALE
                if qi > n_sel:
                    sj = sj + selb[:, j:j + 1]
                m_new = jnp.maximum(m, jnp.max(sj, axis=-1, keepdims=True))
                alpha = jnp.exp(m - m_new)
                p = jnp.exp(sj - m_new)
                l = alpha * l + jnp.sum(p, axis=-1, keepdims=True)
                acc = alpha * acc + jnp.dot(p.astype(BF16), vb_ref[kv, :], preferred_element_type=F32)
                m = m_new
        o_ref[rows, :] = (acc / l * g_ref[rows, :].astype(F32)).astype(o_ref.dtype)


def _moba_prompt(q, k, v, g, bsz, t_len):
    assert t_len % MOBA_BLOCK == 0
    nb = t_len // MOBA_BLOCK
    r3 = lambda a: a.reshape(bsz, t_len, D)
    seq = pl.BlockSpec((None, t_len, DH), lambda b, h: (b, 0, h))
    out = pl.pallas_call(
        functools.partial(_moba_prompt_kernel, nb=nb, n_sel=min(TOP_K, nb - 1)),
        grid=(bsz, H),
        in_specs=[seq, seq, seq, seq],
        out_specs=seq,
        out_shape=jax.ShapeDtypeStruct((bsz, t_len, D), q.dtype),
        scratch_shapes=[pltpu.VMEM((t_len, DH), BF16), pltpu.VMEM((t_len, DH), BF16)],
        compiler_params=_params("parallel", "parallel"),
        name="moba_prompt",
    )(r3(q), r3(k), r3(v), r3(g))
    return out.reshape(bsz * t_len, D)


ATTN_PAGES = 8
PPB = MOBA_BLOCK // PAGE


def _head_rows(ref, h, n):
    return ref[pl.ds(h, n, stride=H), :]


def _moba_sample_kernel(pt_ref, q_ref, ks_ref, vs_ref, g_ref, *rest, ds, n_pages, n_sel):
    del pt_ref
    k_refs, v_refs = rest[:ATTN_PAGES], rest[ATTN_PAGES:2 * ATTN_PAGES]
    o_ref, kbuf, bmean, qbd, selb, m_s, l_s, acc = rest[2 * ATTN_PAGES:]
    i = pl.program_id(1)
    nr = H * ds
    n_full = n_pages // PPB
    steps = n_pages // ATTN_PAGES
    bps = ATTN_PAGES // PPB

    @pl.when(i < steps)
    def _():
        for cb in range(bps):
            blk = i * bps + cb
            heads = []
            for h in range(H):
                tot = jnp.zeros((1, DH), F32)
                for u in range(PPB):
                    kh = _head_rows(k_refs[cb * PPB + u], h, PAGE)
                    r0 = pl.multiple_of((blk * PPB + u) * PAGE, PAGE)
                    kbuf[pl.ds(r0, PAGE), _head(h)] = kh.astype(BF16)
                    tot = tot + jnp.sum(kh, axis=0, keepdims=True)
                heads.append(tot)
            bmean[pl.ds(blk, 1), :] = jnp.concatenate(heads, axis=-1) * (1.0 / MOBA_BLOCK)

    @pl.when(i == steps - 1)
    def _():
        q = q_ref[...]
        lane_h = lax.broadcasted_iota(jnp.int32, (ds, D), 1) // DH
        for h in range(H):
            qbd[h * ds:(h + 1) * ds, :] = jnp.where(lane_h == h, q, 0.0)
        qf = qbd[...]
        ssel = lax.dot_general(qf, bmean[...], NT, precision=HIGHEST, preferred_element_type=F32)
        selb[...] = _select_bias(ssel, n_full, n_sel, axis=1)
        so = lax.dot_general(qf.astype(BF16), ks_ref[...].astype(BF16), NT,
                             preferred_element_type=F32) * SCALE
        trow = lax.broadcasted_iota(jnp.int32, (nr, ds), 0) % ds
        tcol = lax.broadcasted_iota(jnp.int32, (nr, ds), 1)
        so = jnp.where(tcol <= trow, so, NEG_INF)
        m0 = jnp.max(so, axis=-1, keepdims=True)
        p = jnp.exp(so - m0)
        vs = vs_ref[...]
        for h in range(H):
            ph = p[h * ds:(h + 1) * ds, :]
            a = jnp.zeros((ds, DH), F32)
            for t in range(ds):
                a = a + ph[:, t:t + 1] * vs[t:t + 1, _head(h)]
            acc[h * ds:(h + 1) * ds, :] = a
        m_s[...] = m0
        l_s[...] = jnp.sum(p, axis=-1, keepdims=True)

    @pl.when(i >= steps)
    def _():
        qb = qbd[...].astype(BF16)
        nidx = lax.broadcasted_iota(jnp.int32, (nr, n_full), 1)
        sb = selb[...]
        for cb in range(bps):
            blk = (i - steps) * bps + cb
            bias = jnp.max(jnp.where(nidx == blk, sb, NEG_INF), axis=-1, keepdims=True)
            kp = kbuf[pl.ds(pl.multiple_of(blk * MOBA_BLOCK, MOBA_BLOCK), MOBA_BLOCK), :]
            s = lax.dot_general(qb, kp, NT, preferred_element_type=F32) * SCALE + bias
            m_prev = m_s[...]
            m_new = jnp.maximum(m_prev, jnp.max(s, axis=-1, keepdims=True))
            alpha = jnp.exp(m_prev - m_new)
            p = jnp.exp(s - m_new)
            l_s[...] = alpha * l_s[...] + jnp.sum(p, axis=-1, keepdims=True)
            m_s[...] = m_new
            pb = p.astype(BF16)
            for h in range(H):
                hr = slice(h * ds, (h + 1) * ds)
                pv = sum(jnp.dot(pb[hr, u * PAGE:(u + 1) * PAGE],
                                 _head_rows(v_refs[cb * PPB + u], h, PAGE).astype(BF16),
                                 preferred_element_type=F32) for u in range(PPB))
                acc[hr, :] = alpha[hr, :] * acc[hr, :] + pv

    @pl.when(i == 2 * steps - 1)
    def _():
        a = acc[...] / l_s[...]
        g = g_ref[...]
        for h in range(H):
            o_ref[:, _head(h)] = a[h * ds:(h + 1) * ds, :] * g[:, _head(h)]


def _moba_sample(q, ks, vs, g, cache_k, cache_v, page_table, ds):
    db, n_pages = page_table.shape
    assert n_pages % ATTN_PAGES == 0 and ATTN_PAGES % PPB == 0 and ds % 8 == 0
    n_full = n_pages // PPB
    steps = n_pages // ATTN_PAGES
    nr = H * ds
    seq = pl.BlockSpec((None, ds, D), lambda b, i, pt: (b, 0, 0))

    def k_spec(c):
        return pl.BlockSpec((None, PAGE * H, DH),
                            lambda b, i, pt: (pt[b, jnp.minimum(i, steps - 1) * ATTN_PAGES + c], 0, 0))

    def v_spec(c):
        return pl.BlockSpec((None, PAGE * H, DH),
                            lambda b, i, pt: (pt[b, jnp.maximum(i - steps, 0) * ATTN_PAGES + c], 0, 0))

    in_specs = [seq, seq, seq, seq]
    in_specs += [k_spec(c) for c in range(ATTN_PAGES)] + [v_spec(c) for c in range(ATTN_PAGES)]
    r3 = lambda a: a.reshape(db, ds, D)
    out = pl.pallas_call(
        functools.partial(_moba_sample_kernel, ds=ds, n_pages=n_pages, n_sel=min(TOP_K, n_full)),
        grid_spec=pltpu.PrefetchScalarGridSpec(
            num_scalar_prefetch=1,
            grid=(db, 2 * steps),
            in_specs=in_specs,
            out_specs=seq,
            scratch_shapes=[pltpu.VMEM((n_pages * PAGE, D), BF16), pltpu.VMEM((n_full, D), F32),
                            pltpu.VMEM((nr, D), F32), pltpu.VMEM((nr, n_full), F32),
                            pltpu.VMEM((nr, 1), F32), pltpu.VMEM((nr, 1), F32),
                            pltpu.VMEM((nr, DH), F32)],
        ),
        out_shape=jax.ShapeDtypeStruct((db, ds, D), F32),
        compiler_params=_params("parallel", "arbitrary"),
        name="moba_sample",
    )(page_table, r3(q), r3(ks), r3(vs), r3(g),
      *([cache_k] * ATTN_PAGES), *([cache_v] * ATTN_PAGES))
    return out.reshape(db * ds, D)


def _outproj_kernel(x_ref, oa_ref, ob_ref, ma_ref, mb_ref, wa_ref, wb_ref, wo_ref, y_ref):
    ua = jnp.dot(oa_ref[...].astype(BF16), wa_ref[...], preferred_element_type=F32)
    ub = jnp.dot(ob_ref[...].astype(BF16), wb_ref[...], preferred_element_type=F32)
    merged = ma_ref[...].astype(F32) * ua + mb_ref[...].astype(F32) * ub
    y_ref[...] = x_ref[...] + jnp.dot(merged.astype(BF16), wo_ref[...], preferred_element_type=F32)


def _outproj(x2d, oa, ob, ma, mb, wa, wb, wo, tm):
    n = x2d.shape[0]
    assert n % tm == 0
    tok = pl.BlockSpec((tm, D), lambda i: (i, 0))
    wsp = pl.BlockSpec((D, D), lambda i: (0, 0))
    return pl.pallas_call(
        _outproj_kernel,
        grid=(n // tm,),
        in_specs=[tok] * 5 + [wsp] * 3,
        out_specs=tok,
        out_shape=jax.ShapeDtypeStruct((n, D), F32),
        compiler_params=_params("parallel"),
        name="outproj",
    )(x2d, oa, ob, ma, mb, wa, wb, wo)


def _project_in(x2d, nw, w_bf, qn, kn, tm, act):
    qa, fl, ia, ga = _inproj(x2d, nw, w_bf, qn, kn, 0, ("silu", "id", "id", "silu"),
                             (act, F32, act, act), tm)
    qb, k, v, gb = _inproj(x2d, nw, w_bf, qn, kn, 4, ("qnorm", "knorm", "id", "silu"),
                           (act, F32, F32, act), tm)
    ma, mb = _inproj(x2d, nw, w_bf, qn, kn, 8, ("sigmoid", "sigmoid"), (act, act), tm)
    return qa, fl, ia, ga, qb, k, v, gb, ma, mb


def kernel(x_prompt, x_sample, cache_k, cache_v, state_hgrn, page_table, norm_w, w_in, lb_logits,
           onorm_a, q_norm, k_norm, w_proj_a, w_proj_b, w_out):
    assert w_in.shape[0] == 1 and lb_logits.shape[0] == 2, "single-layer step"
    bsz, t_len, _ = x_prompt.shape
    db, ds, _ = x_sample.shape
    n_pool = cache_k.shape[1]
    w_bf = w_in[0].astype(BF16)
    wa, wb, wo = (w[0].astype(BF16) for w in (w_proj_a, w_proj_b, w_out))
    nw, on, qn, kn = norm_w, onorm_a, q_norm, k_norm
    xp = x_prompt.reshape(bsz * t_len, D)
    tm = PROMPT_ROWS
    qa, fl, ia, ga, qb, k_p, v_p, gb, ma, mb = _project_in(xp, nw, w_bf, qn, kn, tm, BF16)
    oa, st_p = _hgrn(qa, fl, ia, ga, lb_logits, on, None, bsz, t_len, HGRN_ROWS)
    ob = _moba_prompt(qb, k_p, v_p, gb, bsz, t_len)
    y_p = _outproj(xp, oa, ob, ma, mb, wa, wb, wo, tm).reshape(bsz, t_len, D)

    xs = x_sample.reshape(db * ds, D)
    tm = min(SAMPLE_ROWS, db * ds)
    qa, fl, ia, ga, qb, k_s, v_s, gb, ma, mb = _project_in(xs, nw, w_bf, qn, kn, tm, F32)
    s0t = jnp.swapaxes(state_hgrn[0].astype(F32), -1, -2)
    oa, st_s = _hgrn(qa, fl, ia, ga, lb_logits, on, s0t, db, ds, ds)
    ck = cache_k[0].reshape(n_pool, PAGE * H, DH)
    cv = cache_v[0].reshape(n_pool, PAGE * H, DH)
    ob = _moba_sample(qb, k_s, v_s, gb, ck, cv, page_table, ds)
    y_s = _outproj(xs, oa, ob, ma, mb, wa, wb, wo, tm).reshape(db, ds, D)

    hd = lambda a, b_, t_: a.reshape(1, b_, t_, H, DH)
    return (y_p, y_s,
            hd(k_p, bsz, t_len), hd(v_p, bsz, t_len), jnp.swapaxes(st_p, -1, -2)[None],
            hd(k_s, db, ds), hd(v_s, db, ds),
            jnp.swapaxes(st_s, -1, -2)[None].astype(state_hgrn.dtype))
```
